```python
import jax, jax.numpy as jnp
from jax import lax
import numpy as np

D_MODEL = 1024
BATCH = 8
SEQ = 4096
DEPTH = 2

N_A_LAYERS = DEPTH // 2
N_B_LAYERS = DEPTH - N_A_LAYERS

DN_ALPHA = (2 * DEPTH) ** 0.25
DN_BETA = (8 * DEPTH) ** -0.25
LN_EPS = 1e-5
RMS_EPS = 1e-6
NEG_INF = -1e30
Q_BLOCK = 128

NSA_HEADS = 16
NSA_KV_GROUPS = 4
NSA_GROUP_SIZE = NSA_HEADS // NSA_KV_GROUPS
NSA_HEAD_DIM = D_MODEL // NSA_HEADS
CMP_LEN = 32
CMP_STRIDE = 16
CMP_HIDDEN = 2 * NSA_HEAD_DIM
SEL_BLOCK = 64
SEL_TOPN = 16
FORCE_BONUS = 1e4
WINDOW = 512
NSA_Q_COLS = NSA_HEADS * NSA_HEAD_DIM
NSA_KV_COLS = NSA_KV_GROUPS * NSA_HEAD_DIM
NSA_IN_COLS = NSA_Q_COLS + 6 * NSA_KV_COLS + 3 * NSA_HEADS

MLA_HEADS = 16
MLA_NOPE = 64
MLA_ROPE = 32
MLA_V = 64
MLA_Q_LORA = 768
MLA_KV_LORA = 256
ROPE_THETA = 10000.0

PEER_HEADS = 8
PEER_NKEYS = 128
PEER_EXPERTS = PEER_NKEYS * PEER_NKEYS
PEER_QDIM = 256
PEER_HALF = PEER_QDIM // 2
PEER_TOPK = 16
PEER_CHUNK = 128

kernel_name = "yoco_nsa_mla_peer_deepnorm"


def layer_norm(x, g, b):
    xf = x.astype(jnp.float32)
    mu = xf.mean(-1, keepdims=True)
    var = jnp.square(xf - mu).mean(-1, keepdims=True)
    y = (xf - mu) * lax.rsqrt(var + LN_EPS)
    return (y * g.astype(jnp.float32) + b.astype(jnp.float32)).astype(x.dtype)


def rms_norm(x, g):
    xf = x.astype(jnp.float32)
    y = xf * lax.rsqrt(jnp.square(xf).mean(-1, keepdims=True) + RMS_EPS)
    return (y * g.astype(jnp.float32)).astype(x.dtype)


def masked_softmax(s, mask, axis=-1):
    s = jnp.where(mask, s.astype(jnp.float32), NEG_INF)
    p = jax.nn.softmax(s, axis=axis)
    return jnp.where(mask, p, 0.0)


def alibi_slopes(n):
    return jnp.asarray([2.0 ** (-8.0 * (h + 1) / n) for h in range(n)], jnp.float32)


def rope_cos_sin(seq, dim):
    inv = 1.0 / (ROPE_THETA ** (jnp.arange(0, dim, 2, dtype=jnp.float32) / dim))
    ang = jnp.arange(seq, dtype=jnp.float32)[:, None] * inv[None, :]
    return jnp.cos(ang), jnp.sin(ang)


def apply_rope(x, cos, sin):
    shp = (cos.shape[0],) + (1,) * (x.ndim - 3) + (cos.shape[1],)
    c, s = cos.reshape(shp), sin.reshape(shp)
    xf = x.astype(jnp.float32)
    x1, x2 = jnp.split(xf, 2, axis=-1)
    return jnp.concatenate([x1 * c - x2 * s, x2 * c + x1 * s], axis=-1).astype(x.dtype)


def cmp_to_sel_matrix(n_cmp, n_sel):
    start = np.arange(n_cmp)[:, None] * CMP_STRIDE
    bstart = np.arange(n_sel)[None, :] * SEL_BLOCK
    ov = np.clip(np.minimum(start + CMP_LEN, bstart + SEL_BLOCK) - np.maximum(start, bstart), 0, None)
    return (ov / CMP_LEN).astype(np.float32)


def nsa_compress(k, w1, w2, pe):
    B, S, G, dh = k.shape
    chunks = k.reshape(B, S // CMP_STRIDE, CMP_STRIDE, G, dh)
    w1r = w1.reshape(2, CMP_STRIDE, dh, CMP_HIDDEN)
    p_first = jnp.einsum('bnlgd,ldh->bngh', chunks, w1r[0])
    p_second = jnp.einsum('bnlgd,ldh->bngh', chunks, w1r[1])
    pe_term = jnp.einsum('ld,ldh->h', pe, w1.reshape(CMP_LEN, dh, CMP_HIDDEN))
    h = jax.nn.gelu(p_first[:, :-1] + p_second[:, 1:] + pe_term)
    return jnp.einsum('bngh,hd->bngd', h, w2)


def nsa_mixer(x, w_in, ck_w1, ck_w2, ck_pe, cv_w1, cv_w2, cv_pe, w_out):
    B, S, _ = x.shape
    G, R, dh = NSA_KV_GROUPS, NSA_GROUP_SIZE, NSA_HEAD_DIM
    proj = x @ w_in
    splits = np.cumsum([NSA_Q_COLS] + [NSA_KV_COLS] * 6).tolist()
    q, kc, vc, ks, vs, kw, vw, gates = jnp.split(proj, splits, axis=-1)
    q = q.reshape(B, S, G, R, dh)
    kc, vc, ks, vs, kw, vw = [a.reshape(B, S, G, dh) for a in (kc, vc, ks, vs, kw, vw)]
    gates = jax.nn.sigmoid(gates.astype(jnp.float32)).reshape(B, S, G, R, 3)

    kc_cmp = nsa_compress(kc, ck_w1, ck_w2, ck_pe)
    vc_cmp = nsa_compress(vc, cv_w1, cv_w2, cv_pe)
    n_cmp = S // CMP_STRIDE - 1
    cmp_end = jnp.arange(n_cmp) * CMP_STRIDE + (CMP_LEN - 1)
    n_sel = S // SEL_BLOCK
    top_n = min(SEL_TOPN, n_sel)
    sel_map = jnp.asarray(cmp_to_sel_matrix(n_cmp, n_sel))
    ks_blocks = ks.reshape(B, n_sel, SEL_BLOCK, G, dh).transpose(0, 3, 1, 2, 4)
    vs_blocks = vs.reshape(B, n_sel, SEL_BLOCK, G, dh).transpose(0, 3, 1, 2, 4)
    kw_pad = jnp.pad(kw, ((0, 0), (WINDOW, 0), (0, 0), (0, 0)))
    vw_pad = jnp.pad(vw, ((0, 0), (WINDOW, 0), (0, 0), (0, 0)))
    slopes = alibi_slopes(NSA_HEADS).reshape(G, R)
    scale = dh ** -0.5
    n_qb = S // Q_BLOCK

    def block_fn(i):
        b = i // n_qb
        q0 = (i % n_qb) * Q_BLOCK
        t = q0 + jnp.arange(Q_BLOCK)
        qb = lax.dynamic_slice(q, (b, q0, 0, 0, 0), (1, Q_BLOCK, G, R, dh))[0]
        gb = lax.dynamic_slice(gates, (b, q0, 0, 0, 0), (1, Q_BLOCK, G, R, 3))[0]
        kcb, vcb = kc_cmp[b], vc_cmp[b]
        dist_c = t[:, None] - cmp_end[None, :]
        s_c = jnp.einsum('tgrd,ngd->grtn', qb, kcb).astype(jnp.float32) * scale \
            - slopes[:, :, None, None] * dist_c.astype(jnp.float32)
        p_c = masked_softmax(s_c, dist_c >= 0)
        o_c = jnp.einsum('grtn,ngd->tgrd', p_c.astype(vcb.dtype), vcb)
        imp = jnp.einsum('grtn,nj->gtj', p_c, sel_map)
        blk_t = t // SEL_BLOCK
        j = jnp.arange(n_sel)
        eligible = j[None, :] <= blk_t[:, None]
        forced = (j[None, :] == 0) | (j[None, :] == blk_t[:, None]) | (j[None, :] == blk_t[:, None] - 1)
        imp = jnp.where(eligible, imp + jnp.where(forced, FORCE_BONUS, 0.0), NEG_INF)
        _, sel_idx = lax.top_k(imp, top_n)
        k_g = jax.vmap(lambda kk, ii: kk[ii])(ks_blocks[b], sel_idx)
        v_g = jax.vmap(lambda vv, ii: vv[ii])(vs_blocks[b], sel_idx)
        pos_s = sel_idx[..., None] * SEL_BLOCK + jnp.arange(SEL_BLOCK)
        dist_s = (t[None, :, None, None] - pos_s)[:, None]
        s_s = jnp.einsum('tgrd,gtnld->grtnl', qb, k_g).astype(jnp.float32) * scale \
            - slopes[:, :, None, None, None] * dist_s.astype(jnp.float32)
        p_s = masked_softmax(s_s, dist_s >= 0, axis=(-2, -1))
        o_s = jnp.einsum('grtnl,gtnld->tgrd', p_s.astype(v_g.dtype), v_g)
        kwb = lax.dynamic_slice(kw_pad, (b, q0, 0, 0), (1, WINDOW + Q_BLOCK, G, dh))[0]
        vwb = lax.dynamic_slice(vw_pad, (b, q0, 0, 0), (1, WINDOW + Q_BLOCK, G, dh))[0]
        s_pos = q0 - WINDOW + jnp.arange(WINDOW + Q_BLOCK)
        dist_w = t[:, None] - s_pos[None, :]
        mask_w = (dist_w >= 0) & (dist_w < WINDOW) & (s_pos[None, :] >= 0)
        s_w = jnp.einsum('tgrd,sgd->grts', qb, kwb).astype(jnp.float32) * scale \
            - slopes[:, :, None, None] * dist_w.astype(jnp.float32)
        p_w = masked_softmax(s_w, mask_w)
        o_w = jnp.einsum('grts,sgd->tgrd', p_w.astype(vwb.dtype), vwb)
        o = gb[..., 0:1] * o_c + gb[..., 1:2] * o_s + gb[..., 2:3] * o_w
        return o.reshape(Q_BLOCK, NSA_Q_COLS).astype(q.dtype)

    out = lax.map(block_fn, jnp.arange(B * n_qb)).reshape(B, S, NSA_Q_COLS)
    return out @ w_out


def mla_shared_kv(x, w_dkv, kv_norm_g, w_kr, w_uk, w_uv, cos, sin):
    c_kv = rms_norm(x @ w_dkv, kv_norm_g)
    k_rope = apply_rope(x @ w_kr, cos, sin)
    k_nope = jnp.einsum('bsc,chd->bshd', c_kv, w_uk)
    v = jnp.einsum('bsc,chd->bshd', c_kv, w_uv)
    return k_nope, k_rope, v


def mla_mixer(x, w_dq, q_norm_g, w_uq, w_out, k_nope, k_rope, v, cos, sin):
    B, S, _ = x.shape
    H = MLA_HEADS
    q = jnp.einsum('bsc,chd->bshd', rms_norm(x @ w_dq, q_norm_g), w_uq)
    q_nope = q[..., :MLA_NOPE]
    q_rope = apply_rope(q[..., MLA_NOPE:], cos, sin)
    scale = (MLA_NOPE + MLA_ROPE) ** -0.5
    n_qb = S // Q_BLOCK
    kpos = jnp.arange(S)

    def block_fn(i):
        b = i // n_qb
        q0 = (i % n_qb) * Q_BLOCK
        t = q0 + jnp.arange(Q_BLOCK)
        qn = lax.dynamic_slice(q_nope, (b, q0, 0, 0), (1, Q_BLOCK, H, MLA_NOPE))[0]
        qr = lax.dynamic_slice(q_rope, (b, q0, 0, 0), (1, Q_BLOCK, H, MLA_ROPE))[0]
        s = (jnp.einsum('thd,shd->hts', qn, k_nope[b]) + jnp.einsum('thd,sd->hts', qr, k_rope[b])).astype(jnp.float32) * scale
        p = masked_softmax(s, kpos[None, :] <= t[:, None])
        vb = v[b]
        o = jnp.einsum('hts,shd->thd', p.astype(vb.dtype), vb)
        return o.reshape(Q_BLOCK, H * MLA_V)

    out = lax.map(block_fn, jnp.arange(B * n_qb)).reshape(B, S, H * MLA_V)
    return out @ w_out


def peer_ffn(x, w_q, subkeys, u, v):
    B, S, D = x.shape
    T, K = PEER_CHUNK, PEER_TOPK
    xt = x.reshape(-1, T, D)

    def chunk_fn(xc):
        q = (xc @ w_q).reshape(T, PEER_HEADS, 2, PEER_HALF)
        s = jnp.einsum('thcd,hckd->thck', q, subkeys).astype(jnp.float32)
        sv, si = lax.top_k(s, K)
        cand = sv[:, :, 0, :, None] + sv[:, :, 1, None, :]
        cidx = si[:, :, 0, :, None] * PEER_NKEYS + si[:, :, 1, None, :]
        cs, cp = lax.top_k(cand.reshape(T, PEER_HEADS, K * K), K)
        eidx = jnp.take_along_axis(cidx.reshape(T, PEER_HEADS, K * K), cp, axis=-1)
        g = jax.nn.softmax(cs, axis=-1)
        ue, ve = u[eidx], v[eidx]
        a = jax.nn.gelu(jnp.einsum('td,thkd->thk', xc, ue).astype(jnp.float32))
        return jnp.einsum('thk,thkd->td', (g * a).astype(xc.dtype), ve)

    return lax.map(chunk_fn, xt).reshape(B, S, D)


def setup_inputs(seed: int = 0) -> dict:
    key = jax.random.key(seed)
    ks = iter(jax.random.split(key, 40))
    D = D_MODEL

    def nrm(shape, scale):
        return jax.random.normal(next(ks), shape, jnp.float32) * scale

    def gain(shape):
        return 1.0 + nrm(shape, 0.02)

    na, nb = N_A_LAYERS, N_B_LAYERS
    flat = CMP_LEN * NSA_HEAD_DIM
    return {
        "x": nrm((BATCH, SEQ, D), 1.0),
        "nsa_w_in": nrm((na, D, NSA_IN_COLS), D ** -0.5),
        "nsa_cmp_k_w1": nrm((na, flat, CMP_HIDDEN), flat ** -0.5),
        "nsa_cmp_k_w2": nrm((na, CMP_HIDDEN, NSA_HEAD_DIM), CMP_HIDDEN ** -0.5),
        "nsa_cmp_k_pe": nrm((na, CMP_LEN, NSA_HEAD_DIM), 0.1),
        "nsa_cmp_v_w1": nrm((na, flat, CMP_HIDDEN), flat ** -0.5),
        "nsa_cmp_v_w2": nrm((na, CMP_HIDDEN, NSA_HEAD_DIM), CMP_HIDDEN ** -0.5),
        "nsa_cmp_v_pe": nrm((na, CMP_LEN, NSA_HEAD_DIM), 0.1),
        "nsa_w_out": nrm((na, NSA_Q_COLS, D), DN_BETA * NSA_Q_COLS ** -0.5),
        "nsa_ln_g": gain((na, D)),
        "nsa_ln_b": nrm((na, D), 0.02),
        "mla_w_dkv": nrm((D, MLA_KV_LORA), D ** -0.5),
        "mla_kv_norm_g": gain((MLA_KV_LORA,)),
        "mla_w_kr": nrm((D, MLA_ROPE), D ** -0.5),
        "mla_w_uk": nrm((MLA_KV_LORA, MLA_HEADS, MLA_NOPE), MLA_KV_LORA ** -0.5),
        "mla_w_uv": nrm((MLA_KV_LORA, MLA_HEADS, MLA_V), MLA_KV_LORA ** -0.5),
        "mla_w_dq": nrm((nb, D, MLA_Q_LORA), D ** -0.5),
        "mla_q_norm_g": gain((nb, MLA_Q_LORA)),
        "mla_w_uq": nrm((nb, MLA_Q_LORA, MLA_HEADS, MLA_NOPE + MLA_ROPE), MLA_Q_LORA ** -0.5),
        "mla_w_out": nrm((nb, MLA_HEADS * MLA_V, D), DN_BETA * (MLA_HEADS * MLA_V) ** -0.5),
        "mla_ln_g": gain((nb, D)),
        "mla_ln_b": nrm((nb, D), 0.02),
        "peer_w_q": nrm((DEPTH, D, PEER_HEADS * PEER_QDIM), D ** -0.5),
        "peer_subkeys": nrm((DEPTH, PEER_HEADS, 2, PEER_NKEYS, PEER_HALF), PEER_HALF ** -0.5),
        "peer_u": nrm((DEPTH, PEER_EXPERTS, D), D ** -0.5),
        "peer_v": nrm((DEPTH, PEER_EXPERTS, D), DN_BETA),
        "peer_ln_g": gain((DEPTH, D)),
        "peer_ln_b": nrm((DEPTH, D), 0.02),
    }


def reference(x, nsa_w_in, nsa_cmp_k_w1, nsa_cmp_k_w2, nsa_cmp_k_pe, nsa_cmp_v_w1, nsa_cmp_v_w2,
              nsa_cmp_v_pe, nsa_w_out, nsa_ln_g, nsa_ln_b, mla_w_dkv, mla_kv_norm_g, mla_w_kr,
              mla_w_uk, mla_w_uv, mla_w_dq, mla_q_norm_g, mla_w_uq, mla_w_out, mla_ln_g, mla_ln_b,
              peer_w_q, peer_subkeys, peer_u, peer_v, peer_ln_g, peer_ln_b):
    S = x.shape[1]
    cos, sin = rope_cos_sin(S, MLA_ROPE)
    h = x
    k_nope = k_rope = v_shared = None
    for layer in range(DEPTH):
        if layer < N_A_LAYERS:
            a = layer
            mix = nsa_mixer(h, nsa_w_in[a], nsa_cmp_k_w1[a], nsa_cmp_k_w2[a], nsa_cmp_k_pe[a],
                            nsa_cmp_v_w1[a], nsa_cmp_v_w2[a], nsa_cmp_v_pe[a], nsa_w_out[a])
            h = layer_norm(DN_ALPHA * h + mix, nsa_ln_g[a], nsa_ln_b[a])
        else:
            if layer == N_A_LAYERS:
                k_nope, k_rope, v_shared = mla_shared_kv(h, mla_w_dkv, mla_kv_norm_g, mla_w_kr,
                                                         mla_w_uk, mla_w_uv, cos, sin)
            bi = layer - N_A_LAYERS
            mix = mla_mixer(h, mla_w_dq[bi], mla_q_norm_g[bi], mla_w_uq[bi], mla_w_out[bi],
                            k_nope, k_rope, v_shared, cos, sin)
            h = layer_norm(DN_ALPHA * h + mix, mla_ln_g[bi], mla_ln_b[bi])
        ffn = peer_ffn(h, peer_w_q[layer], peer_subkeys[layer], peer_u[layer], peer_v[layer])
        h = layer_norm(DN_ALPHA * h + ffn, peer_ln_g[layer], peer_ln_b[layer])
    return h
```

```python
import functools
import math

import numpy as np
import jax
import jax.numpy as jnp
from jax import lax
from jax.experimental import pallas as pl
from jax.experimental.pallas import tpu as pltpu

F32 = jnp.float32
BF16 = jnp.bfloat16

D_MODEL = 1024
DEPTH = 2
N_A_LAYERS = DEPTH // 2
DN_ALPHA = (2 * DEPTH) ** 0.25
LN_EPS = 1e-5
RMS_EPS = 1e-6
NEG_INF = -1e30

NSA_HEADS = 16
NSA_KV_GROUPS = 4
NSA_GROUP_SIZE = NSA_HEADS // NSA_KV_GROUPS
NSA_HEAD_DIM = D_MODEL // NSA_HEADS
CMP_LEN = 32
CMP_STRIDE = 16
CMP_HIDDEN = 2 * NSA_HEAD_DIM
SEL_BLOCK = 64
SEL_TOPN = 16
FORCE_BONUS = 1e4
WINDOW = 512
NSA_Q_COLS = NSA_HEADS * NSA_HEAD_DIM
NSA_KV_COLS = NSA_KV_GROUPS * NSA_HEAD_DIM

MLA_HEADS = 16
MLA_NOPE = 64
MLA_ROPE = 32
MLA_V = 64
MLA_Q_LORA = 768
MLA_KV_LORA = 256
ROPE_THETA = 10000.0

PEER_HEADS = 8
PEER_NKEYS = 128
PEER_QDIM = 256
PEER_HALF = PEER_QDIM // 2
PEER_TOPK = 16
PEER_SLOTS = PEER_HEADS * PEER_TOPK

LANES = 128
V7X_VMEM_LIMIT_BYTES = 48 * 1024 * 1024

NSA_TQ = 128
NSA_TK = 512
MLA_TQ = 512
PEER_ROUTE_TM = 256
PEER_TT = 8


def _cparams(sem):
    return pltpu.CompilerParams(dimension_semantics=sem, vmem_limit_bytes=V7X_VMEM_LIMIT_BYTES)


def _dot(a, b):
    return jnp.dot(a, b, preferred_element_type=F32)


def _dot_nt(a, b):
    return lax.dot_general(a, b, (((1,), (1,)), ((), ())), preferred_element_type=F32)


def _gelu_tanh(x):
    return 0.5 * x * (1.0 + jnp.tanh(math.sqrt(2.0 / math.pi) * (x + 0.044715 * (x * x * x))))


def _proj_kernel(a_ref, w_ref, o_ref):
    o_ref[...] = _dot(a_ref[...].astype(BF16), w_ref[...]).astype(o_ref.dtype)


def _proj(a, w, out_dtype, tm=512):
    m, k = a.shape
    nc = w.shape[1]
    return pl.pallas_call(
        _proj_kernel,
        out_shape=jax.ShapeDtypeStruct((m, nc), out_dtype),
        grid=(m // tm,),
        in_specs=[pl.BlockSpec((tm, k), lambda i: (i, 0)), pl.BlockSpec((k, nc), lambda i: (0, 0))],
        out_specs=pl.BlockSpec((tm, nc), lambda i: (i, 0)),
        compiler_params=_cparams(("parallel",)),
        name="proj",
    )(a, w)


def _layer_norm_rows(y, g, b):
    mu = jnp.mean(y, axis=-1, keepdims=True)
    d = y - mu
    var = jnp.mean(d * d, axis=-1, keepdims=True)
    return d * lax.rsqrt(var + LN_EPS) * g + b


def _proj_ln_kernel(a_ref, w_ref, r_ref, g_ref, b_ref, o_ref):
    mix = _dot(a_ref[...].astype(BF16), w_ref[...])
    o_ref[...] = _layer_norm_rows(DN_ALPHA * r_ref[...] + mix, g_ref[...], b_ref[...])


def _proj_ln(a, w, resid, g, b, tm=256):
    m, k = a.shape
    d = w.shape[1]
    return pl.pallas_call(
        _proj_ln_kernel,
        out_shape=jax.ShapeDtypeStruct((m, d), F32),
        grid=(m // tm,),
        in_specs=[pl.BlockSpec((tm, k), lambda i: (i, 0)), pl.BlockSpec((k, d), lambda i: (0, 0)),
                  pl.BlockSpec((tm, d), lambda i: (i, 0)), pl.BlockSpec((1, d), lambda i: (0, 0)),
                  pl.BlockSpec((1, d), lambda i: (0, 0))],
        out_specs=pl.BlockSpec((tm, d), lambda i: (i, 0)),
        compiler_params=_cparams(("parallel",)),
        name="proj_ln",
    )(a, w, resid, g.reshape(1, d), b.reshape(1, d))


def _proj_rms_kernel(a_ref, w_ref, g_ref, o_ref):
    y = _dot(a_ref[...].astype(BF16), w_ref[...])
    ms = jnp.mean(y * y, axis=-1, keepdims=True)
    o_ref[...] = (y * lax.rsqrt(ms + RMS_EPS) * g_ref[...]).astype(o_ref.dtype)


def _proj_rms(a, w, g, tm=512):
    m, k = a.shape
    d = w.shape[1]
    return pl.pallas_call(
        _proj_rms_kernel,
        out_shape=jax.ShapeDtypeStruct((m, d), F32),
        grid=(m // tm,),
        in_specs=[pl.BlockSpec((tm, k), lambda i: (i, 0)), pl.BlockSpec((k, d), lambda i: (0, 0)),
                  pl.BlockSpec((1, d), lambda i: (0, 0))],
        out_specs=pl.BlockSpec((tm, d), lambda i: (i, 0)),
        compiler_params=_cparams(("parallel",)),
        name="proj_rms",
    )(a, w, g.reshape(1, d))


def _proj_rope_kernel(a_ref, w_ref, cos_ref, sin_ref, o_ref, *, n_blocks):
    y = _dot(a_ref[...].astype(BF16), w_ref[...])
    c = cos_ref[...]
    s = sin_ref[...]
    for hb in range(n_blocks):
        blk = y[:, hb * LANES:(hb + 1) * LANES]
        o_ref[:, hb * LANES:(hb + 1) * LANES] = (blk * c + pltpu.roll(blk, LANES - MLA_ROPE, 1) * s).astype(o_ref.dtype)


def _proj_rope(a, w, cos_t, sin_t, seq, out_dtype, tm=512):
    m, k = a.shape
    nc = w.shape[1]
    n_pos_tiles = seq // tm
    return pl.pallas_call(
        functools.partial(_proj_rope_kernel, n_blocks=nc // LANES),
        out_shape=jax.ShapeDtypeStruct((m, nc), out_dtype),
        grid=(m // tm,),
        in_specs=[pl.BlockSpec((tm, k), lambda i: (i, 0)), pl.BlockSpec((k, nc), lambda i: (0, 0)),
                  pl.BlockSpec((tm, LANES), lambda i: (i % n_pos_tiles, 0)),
                  pl.BlockSpec((tm, LANES), lambda i: (i % n_pos_tiles, 0))],
        out_specs=pl.BlockSpec((tm, nc), lambda i: (i, 0)),
        compiler_params=_cparams(("parallel",)),
        name="proj_rope",
    )(a, w, cos_t, sin_t)


def _nsa_compress_kernel(c_ref, w1a_ref, w1b_ref, w1_ref, pe_ref, w2_ref, o_ref):
    nc = c_ref.shape[1]
    c = c_ref[0].astype(BF16)
    first = _dot(c, w1a_ref[...])
    second = _dot(c, w1b_ref[...])
    pe_term = _dot(pe_ref[...], w1_ref[...])[0:1]
    second_next = pltpu.roll(second, nc - 1, 0)
    h = _gelu_tanh(first + second_next + pe_term)
    out = _dot(h.astype(BF16), w2_ref[...])
    row = lax.broadcasted_iota(jnp.int32, out.shape, 0)
    o_ref[0] = jnp.where(row < nc - 1, out, 0.0).astype(o_ref.dtype)


def _nsa_compress(chunks, w1, w2, pe):
    bg, nc, flat = chunks.shape
    w1b16 = w1.astype(BF16)
    pe8 = jnp.broadcast_to(pe.reshape(1, CMP_LEN * NSA_HEAD_DIM), (8, CMP_LEN * NSA_HEAD_DIM)).astype(BF16)
    full = lambda shape: pl.BlockSpec(shape, lambda i: (0,) * len(shape))
    return pl.pallas_call(
        _nsa_compress_kernel,
        out_shape=jax.ShapeDtypeStruct((bg, nc, NSA_HEAD_DIM), BF16),
        grid=(bg,),
        in_specs=[pl.BlockSpec((1, nc, flat), lambda i: (i, 0, 0)),
                  full((flat, CMP_HIDDEN)), full((flat, CMP_HIDDEN)), full((2 * flat, CMP_HIDDEN)),
                  full((8, 2 * flat)), full((CMP_HIDDEN, NSA_HEAD_DIM))],
        out_specs=pl.BlockSpec((1, nc, NSA_HEAD_DIM), lambda i: (i, 0, 0)),
        compiler_params=_cparams(("parallel",)),
        name="nsa_compress",
    )(chunks, w1b16[:flat], w1b16[flat:], w1b16, pe8, w2.astype(BF16))


def _nsa_attn_kernel(q_ref, kc_ref, vc_ref, ks_ref, vs_ref, kw_ref, vw_ref, gate_ref, slope_ref,
                     selmap_ref, expand_ref, o_ref, *, top_n):
    tq = q_ref.shape[2]
    rows = NSA_GROUP_SIZE * tq
    ncp = kc_ref.shape[2]
    nsel = selmap_ref.shape[1]
    tk = expand_ref.shape[2]
    scale = NSA_HEAD_DIM ** -0.5
    i = pl.program_id(2)
    q0 = i * tq

    q4 = q_ref[0].reshape(rows, NSA_HEAD_DIM)
    slope = slope_ref[0]
    t = q0 + (lax.broadcasted_iota(jnp.int32, (rows, 1), 0) & (tq - 1))

    n = lax.broadcasted_iota(jnp.int32, (1, ncp), 1)
    dist_c = t - (n * CMP_STRIDE + (CMP_LEN - 1))
    valid_c = (dist_c >= 0) & (n < ncp - 1)
    s_c = _dot_nt(q4, kc_ref[0, 0]) * scale - slope * dist_c.astype(F32)
    s_c = jnp.where(valid_c, s_c, NEG_INF)
    m_c = jnp.max(s_c, axis=-1, keepdims=True)
    e_c = jnp.where(valid_c, jnp.exp(s_c - m_c), 0.0)
    den_c = jnp.sum(e_c, axis=-1, keepdims=True)
    p_c = e_c / jnp.where(den_c > 0.0, den_c, 1.0)
    o_c = _dot(p_c.astype(BF16), vc_ref[0, 0])

    p_grp = p_c[0:tq] + p_c[tq:2 * tq] + p_c[2 * tq:3 * tq] + p_c[3 * tq:4 * tq]
    imp = _dot(p_grp.astype(BF16), selmap_ref[...])
    tq_pos = q0 + lax.broadcasted_iota(jnp.int32, (tq, 1), 0)
    blk_t = tq_pos >> int(math.log2(SEL_BLOCK))
    j = lax.broadcasted_iota(jnp.int32, (1, nsel), 1)
    eligible = j <= blk_t
    forced = (j == 0) | (j == blk_t) | (j == blk_t - 1)
    work = jnp.where(eligible, imp + jnp.where(forced, FORCE_BONUS, 0.0), NEG_INF)
    jf = j.astype(F32)
    sel = jnp.zeros((tq, nsel), F32)
    for _ in range(top_n):
        m = jnp.max(work, axis=-1, keepdims=True)
        pos = jnp.min(jnp.where(work == m, jf, float(nsel)), axis=-1, keepdims=True)
        hit = jf == pos
        sel = jnp.where(hit, 1.0, sel)
        work = jnp.where(hit, -jnp.inf, work)
    sel4 = jnp.concatenate([sel] * NSA_GROUP_SIZE, axis=0).astype(BF16)

    def sel_step(kt, carry):
        m_i, l_i, acc = carry
        k0 = pl.multiple_of(kt * tk, tk)
        k = ks_ref[0, 0, pl.ds(k0, tk), :]
        v = vs_ref[0, 0, pl.ds(k0, tk), :]
        chosen = _dot(sel4, expand_ref[kt])
        dist = t - (k0 + lax.broadcasted_iota(jnp.int32, (1, tk), 1))
        valid = (chosen > 0.5) & (dist >= 0)
        s = _dot_nt(q4, k) * scale - slope * dist.astype(F32)
        s = jnp.where(valid, s, NEG_INF)
        m_new = jnp.maximum(m_i, jnp.max(s, axis=-1, keepdims=True))
        alpha = jnp.exp(m_i - m_new)
        p = jnp.where(valid, jnp.exp(s - m_new), 0.0)
        l_new = alpha * l_i + jnp.sum(p, axis=-1, keepdims=True)
        acc_new = alpha * acc + _dot(p.astype(BF16), v)
        return m_new, l_new, acc_new

    n_kt = (q0 + tq - 1) // tk + 1
    init = (jnp.full((rows, 1), NEG_INF, F32), jnp.zeros((rows, 1), F32), jnp.zeros((rows, NSA_HEAD_DIM), F32))
    _, l_s, acc_s = lax.fori_loop(0, n_kt, sel_step, init)
    o_s = acc_s / l_s

    wk = WINDOW + tq
    w0 = pl.multiple_of(jnp.maximum(q0 - WINDOW, 0), tq)
    kwin = kw_ref[0, 0, pl.ds(w0, wk), :]
    vwin = vw_ref[0, 0, pl.ds(w0, wk), :]
    dist_w = t - (w0 + lax.broadcasted_iota(jnp.int32, (1, wk), 1))
    valid_w = (dist_w >= 0) & (dist_w < WINDOW)
    s_w = _dot_nt(q4, kwin) * scale - slope * dist_w.astype(F32)
    s_w = jnp.where(valid_w, s_w, NEG_INF)
    m_w = jnp.max(s_w, axis=-1, keepdims=True)
    e_w = jnp.where(valid_w, jnp.exp(s_w - m_w), 0.0)
    o_w = _dot(e_w.astype(BF16), vwin) / jnp.sum(e_w, axis=-1, keepdims=True)

    gate = jax.nn.sigmoid(gate_ref[0].reshape(rows, 3))
    o = gate[:, 0:1] * o_c + gate[:, 1:2] * o_s + gate[:, 2:3] * o_w
    o_ref[0] = o.reshape(NSA_GROUP_SIZE, tq, NSA_HEAD_DIM).astype(o_ref.dtype)


def _cmp_to_sel_matrix(n_cmp_padded, n_sel):
    start = np.arange(n_cmp_padded)[:, None] * CMP_STRIDE
    bstart = np.arange(n_sel)[None, :] * SEL_BLOCK
    ov = np.clip(np.minimum(start + CMP_LEN, bstart + SEL_BLOCK) - np.maximum(start, bstart), 0, None)
    m = (ov / CMP_LEN).astype(np.float32)
    m[n_cmp_padded - 1] = 0.0
    return m


def _nsa_attention(q_t, kc_cmp, vc_cmp, ks_t, vs_t, kw_t, vw_t, gates_t):
    b, _, s, _ = q_t.shape
    g = NSA_KV_GROUPS
    tq, tk = NSA_TQ, min(NSA_TK, s)
    ncp = s // CMP_STRIDE
    nsel = s // SEL_BLOCK
    rows = NSA_GROUP_SIZE * tq
    slopes = np.asarray([2.0 ** (-8.0 * (h + 1) / NSA_HEADS) for h in range(NSA_HEADS)], np.float32)
    slope_rows = jnp.asarray(np.repeat(slopes.reshape(g, NSA_GROUP_SIZE), tq, axis=1).reshape(g, rows, 1))
    selmap = jnp.asarray(_cmp_to_sel_matrix(ncp, nsel)).astype(BF16)
    key_blk = (np.arange(s) // SEL_BLOCK).reshape(s // tk, 1, tk)
    expand = jnp.asarray((key_blk == np.arange(nsel).reshape(1, nsel, 1)).astype(np.float32)).astype(BF16)
    kv_spec = pl.BlockSpec((1, 1, s, NSA_HEAD_DIM), lambda bi, gi, i: (bi, gi, 0, 0))
    cmp_spec = pl.BlockSpec((1, 1, ncp, NSA_HEAD_DIM), lambda bi, gi, i: (bi, gi, 0, 0))
    return pl.pallas_call(
        functools.partial(_nsa_attn_kernel, top_n=min(SEL_TOPN, nsel)),
        out_shape=jax.ShapeDtypeStruct((b, NSA_HEADS, s, NSA_HEAD_DIM), BF16),
        grid=(b, g, s // tq),
        in_specs=[pl.BlockSpec((1, NSA_GROUP_SIZE, tq, NSA_HEAD_DIM), lambda bi, gi, i: (bi, gi, i, 0)),
                  cmp_spec, cmp_spec, kv_spec, kv_spec, kv_spec, kv_spec,
                  pl.BlockSpec((1, NSA_GROUP_SIZE, tq, 3), lambda bi, gi, i: (bi, gi, i, 0)),
                  pl.BlockSpec((1, rows, 1), lambda bi, gi, i: (gi, 0, 0)),
                  pl.BlockSpec((ncp, nsel), lambda bi, gi, i: (0, 0)),
                  pl.BlockSpec((s // tk, nsel, tk), lambda bi, gi, i: (0, 0, 0))],
        out_specs=pl.BlockSpec((1, NSA_GROUP_SIZE, tq, NSA_HEAD_DIM), lambda bi, gi, i: (bi, gi, i, 0)),
        compiler_params=_cparams(("parallel", "parallel", "parallel")),
        name="nsa_attention",
    )(q_t, kc_cmp, vc_cmp, ks_t, vs_t, kw_t, vw_t, gates_t, slope_rows, selmap, expand)


def _heads_major(a, b, s, nh, hd):
    return a.reshape(b, s, nh, hd).transpose(0, 2, 1, 3)


def _nsa_layer(h, b, s, w_in, ck_w1, ck_w2, ck_pe, cv_w1, cv_w2, cv_pe, w_out, ln_g, ln_b):
    g = NSA_KV_GROUPS
    n_gate = 3 * NSA_HEADS
    main_cols = NSA_Q_COLS + 6 * NSA_KV_COLS
    w_pad = jnp.pad(w_in, ((0, 0), (0, LANES - n_gate))).astype(BF16)
    proj = _proj(h, w_pad, F32)
    q = proj[:, :NSA_Q_COLS]
    parts = [proj[:, NSA_Q_COLS + c * NSA_KV_COLS: NSA_Q_COLS + (c + 1) * NSA_KV_COLS] for c in range(6)]
    kc, vc, ks, vs, kw, vw = parts
    gates = proj[:, main_cols:main_cols + n_gate]

    def chunks(a):
        a = a.reshape(b, s // CMP_STRIDE, CMP_STRIDE, g, NSA_HEAD_DIM).transpose(0, 3, 1, 2, 4)
        return a.reshape(b * g, s // CMP_STRIDE, CMP_STRIDE * NSA_HEAD_DIM)

    kc_cmp = _nsa_compress(chunks(kc), ck_w1, ck_w2, ck_pe).reshape(b, g, s // CMP_STRIDE, NSA_HEAD_DIM)
    vc_cmp = _nsa_compress(chunks(vc), cv_w1, cv_w2, cv_pe).reshape(b, g, s // CMP_STRIDE, NSA_HEAD_DIM)
    q_t = _heads_major(q, b, s, NSA_HEADS, NSA_HEAD_DIM).astype(BF16)
    ks_t, vs_t, kw_t, vw_t = [_heads_major(a, b, s, g, NSA_HEAD_DIM).astype(BF16) for a in (ks, vs, kw, vw)]
    gates_t = _heads_major(gates, b, s, NSA_HEADS, 3)
    o_t = _nsa_attention(q_t, kc_cmp, vc_cmp, ks_t, vs_t, kw_t, vw_t, gates_t)
    o = o_t.transpose(0, 2, 1, 3).reshape(b * s, NSA_Q_COLS)
    return _proj_ln(o, w_out.astype(BF16), h, ln_g, ln_b)


def _mla_attn_kernel(q_ref, k_ref, v_ref, o_ref):
    tq = q_ref.shape[2]
    tk = tq
    scale = (MLA_NOPE + MLA_ROPE) ** -0.5
    i = pl.program_id(2)
    q = q_ref[0, 0]
    t = i * tq + lax.broadcasted_iota(jnp.int32, (tq, 1), 0)

    def step(kt, carry):
        m_i, l_i, acc = carry
        k0 = pl.multiple_of(kt * tk, tk)
        k = k_ref[0, 0, pl.ds(k0, tk), :]
        v = v_ref[0, 0, pl.ds(k0, tk), :]
        valid = (k0 + lax.broadcasted_iota(jnp.int32, (1, tk), 1)) <= t
        s = jnp.where(valid, _dot_nt(q, k) * scale, NEG_INF)
        m_new = jnp.maximum(m_i, jnp.max(s, axis=-1, keepdims=True))
        alpha = jnp.exp(m_i - m_new)
        p = jnp.where(valid, jnp.exp(s - m_new), 0.0)
        l_new = alpha * l_i + jnp.sum(p, axis=-1, keepdims=True)
        acc_new = alpha * acc + _dot(p.astype(BF16), v)
        return m_new, l_new, acc_new

    init = (jnp.full((tq, 1), NEG_INF, F32), jnp.zeros((tq, 1), F32), jnp.zeros((tq, MLA_V), F32))
    _, l, acc = lax.fori_loop(0, i + 1, step, init)
    o_ref[0, 0] = (acc / l).astype(o_ref.dtype)


def _mla_attention(q_t, k_t, v_t):
    b, nh, s, dk = q_t.shape
    tq = min(MLA_TQ, s)
    return pl.pallas_call(
        _mla_attn_kernel,
        out_shape=jax.ShapeDtypeStruct((b, nh, s, MLA_V), BF16),
        grid=(b, nh, s // tq),
        in_specs=[pl.BlockSpec((1, 1, tq, dk), lambda bi, hi, i: (bi, hi, i, 0)),
                  pl.BlockSpec((1, 1, s, dk), lambda bi, hi, i: (bi, hi, 0, 0)),
                  pl.BlockSpec((1, 1, s, MLA_V), lambda bi, hi, i: (bi, hi, 0, 0))],
        out_specs=pl.BlockSpec((1, 1, tq, MLA_V), lambda bi, hi, i: (bi, hi, i, 0)),
        compiler_params=_cparams(("parallel", "parallel", "parallel")),
        name="mla_attention",
    )(q_t, k_t, v_t)


def _rotate_half_cols(w):
    half = MLA_ROPE // 2
    return jnp.concatenate([-w[..., half:], w[..., :half]], axis=-1)


def _rope_tables(seq, lead):
    inv = 1.0 / (ROPE_THETA ** (jnp.arange(0, MLA_ROPE, 2, dtype=F32) / MLA_ROPE))
    ang = jnp.arange(seq, dtype=F32)[:, None] * inv[None, :]
    cos2 = jnp.concatenate([jnp.cos(ang), jnp.cos(ang)], axis=-1)
    sin2 = jnp.concatenate([jnp.sin(ang), jnp.sin(ang)], axis=-1)
    tail = LANES - lead - MLA_ROPE
    cos_t = jnp.concatenate([jnp.ones((seq, lead), F32), cos2, jnp.zeros((seq, tail), F32)], axis=-1)
    sin_t = jnp.concatenate([jnp.zeros((seq, lead), F32), sin2, jnp.zeros((seq, tail), F32)], axis=-1)
    return cos_t, sin_t


def _mla_shared_kv(h, b, s, w_dkv, kv_norm_g, w_kr, w_uk, w_uv):
    nh = MLA_HEADS
    c_kv = _proj_rms(h, w_dkv.astype(BF16), kv_norm_g)
    w_k = jnp.concatenate([w_kr, _rotate_half_cols(w_kr), jnp.zeros((D_MODEL, LANES - 2 * MLA_ROPE), F32)], axis=-1)
    cos_k, sin_k = _rope_tables(s, 0)
    k_rope = _proj_rope(h, w_k.astype(BF16), cos_k, sin_k, s, BF16)[:, :MLA_ROPE]
    w_up = jnp.concatenate([w_uk.reshape(MLA_KV_LORA, nh * MLA_NOPE), w_uv.reshape(MLA_KV_LORA, nh * MLA_V)], axis=-1)
    kv = _proj(c_kv, w_up.astype(BF16), BF16)
    k_nope = kv[:, :nh * MLA_NOPE].reshape(b, s, nh, MLA_NOPE)
    v = kv[:, nh * MLA_NOPE:].reshape(b, s, nh, MLA_V)
    k_rope_h = jnp.broadcast_to(k_rope.reshape(b, s, 1, MLA_ROPE), (b, s, nh, MLA_ROPE))
    pad = jnp.zeros((b, s, nh, LANES - MLA_NOPE - MLA_ROPE), BF16)
    k_t = jnp.concatenate([k_nope, k_rope_h, pad], axis=-1).transpose(0, 2, 1, 3)
    return k_t, v.transpose(0, 2, 1, 3)


def _mla_layer(h, b, s, k_t, v_t, w_dq, q_norm_g, w_uq, w_out, ln_g, ln_b):
    nh = MLA_HEADS
    c_q = _proj_rms(h, w_dq.astype(BF16), q_norm_g)
    w_nope, w_rope = w_uq[..., :MLA_NOPE], w_uq[..., MLA_NOPE:]
    w_q = jnp.concatenate([w_nope, w_rope, _rotate_half_cols(w_rope)], axis=-1).reshape(MLA_Q_LORA, nh * LANES)
    cos_q, sin_q = _rope_tables(s, MLA_NOPE)
    q = _proj_rope(c_q, w_q.astype(BF16), cos_q, sin_q, s, BF16)
    q_t = q.reshape(b, s, nh, LANES).transpose(0, 2, 1, 3)
    o_t = _mla_attention(q_t, k_t, v_t)
    o = o_t.transpose(0, 2, 1, 3).reshape(b * s, nh * MLA_V)
    return _proj_ln(o, w_out.astype(BF16), h, ln_g, ln_b)


def _topk_rows(x, k, payload=None):
    n = x.shape[0]
    rows = lax.broadcasted_iota(jnp.int32, x.shape, 0).astype(F32)
    vals, picks = [], []
    for _ in range(k):
        m = jnp.max(x, axis=0, keepdims=True)
        pos = jnp.min(jnp.where(x == m, rows, float(n)), axis=0, keepdims=True)
        hit = rows == pos
        vals.append(m)
        picks.append(pos if payload is None else jnp.sum(jnp.where(hit, payload, 0.0), axis=0, keepdims=True))
        x = jnp.where(hit, -jnp.inf, x)
    return jnp.concatenate(vals, axis=0), jnp.concatenate(picks, axis=0)


def _peer_route_kernel(h_ref, wq_ref, sk_ref, eidx_ref, gate_ref, q_scr):
    k = PEER_TOPK
    q_scr[...] = _dot(h_ref[...].astype(BF16), wq_ref[...])

    def head(hh, carry):
        off = pl.multiple_of(hh * PEER_QDIM, PEER_QDIM)
        qa = q_scr[:, pl.ds(off, PEER_HALF)].astype(BF16)
        qb = q_scr[:, pl.ds(off + PEER_HALF, PEER_HALF)].astype(BF16)
        sa = _dot_nt(sk_ref[2 * hh], qa)
        sb = _dot_nt(sk_ref[2 * hh + 1], qb)
        va, ia = _topk_rows(sa, k)
        vb, ib = _topk_rows(sb, k)
        cand = jnp.concatenate([va[r:r + 1] + vb for r in range(k)], axis=0)
        cidx = jnp.concatenate([ia[r:r + 1] * float(PEER_NKEYS) + ib for r in range(k)], axis=0)
        cs, ce = _topk_rows(cand, k, payload=cidx)
        e = jnp.exp(cs - jnp.max(cs, axis=0, keepdims=True))
        row0 = pl.multiple_of(hh * k, k)
        gate_ref[pl.ds(row0, k), :] = e / jnp.sum(e, axis=0, keepdims=True)
        eidx_ref[pl.ds(row0, k), :] = ce.astype(jnp.int32)
        return carry

    lax.fori_loop(0, PEER_HEADS, head, 0)


def _peer_route(h, w_q, subkeys):
    n = h.shape[0]
    tm = PEER_ROUTE_TM
    qcols = PEER_HEADS * PEER_QDIM
    sk = subkeys.reshape(2 * PEER_HEADS, PEER_NKEYS, PEER_HALF).astype(BF16)
    return pl.pallas_call(
        _peer_route_kernel,
        out_shape=(jax.ShapeDtypeStruct((PEER_SLOTS, n), jnp.int32), jax.ShapeDtypeStruct((PEER_SLOTS, n), F32)),
        grid=(n // tm,),
        in_specs=[pl.BlockSpec((tm, D_MODEL), lambda i: (i, 0)),
                  pl.BlockSpec((D_MODEL, qcols), lambda i: (0, 0)),
                  pl.BlockSpec((2 * PEER_HEADS, PEER_NKEYS, PEER_HALF), lambda i: (0, 0, 0))],
        out_specs=(pl.BlockSpec((PEER_SLOTS, tm), lambda i: (0, i)), pl.BlockSpec((PEER_SLOTS, tm), lambda i: (0, i))),
        scratch_shapes=[pltpu.VMEM((tm, qcols), F32)],
        compiler_params=_cparams(("parallel",)),
        name="peer_route",
    )(h, w_q.astype(BF16), sk)


def _peer_expert_kernel(idx_cur_ref, idx_nxt_ref, x_ref, g2_ref, table_ref, lng_ref, lnb_ref, o_ref, buf, sem):
    tt = x_ref.shape[0]
    n_rows = buf.shape[1]
    step = pl.program_id(0)
    slot = step % 2

    def row_copy(idx_ref, r, dst_slot):
        return pltpu.make_async_copy(table_ref.at[pl.ds(idx_ref[0, 0, r], 1)], buf.at[dst_slot, pl.ds(r, 1)],
                                     sem.at[dst_slot])

    def issue(idx_ref, dst_slot):
        def body(r, carry):
            row_copy(idx_ref, r, dst_slot).start()
            return carry
        lax.fori_loop(0, n_rows, body, 0, unroll=8)

    @pl.when(step == 0)
    def _():
        issue(idx_cur_ref, 0)

    @pl.when(step + 1 < pl.num_programs(0))
    def _():
        issue(idx_nxt_ref, 1 - slot)

    pltpu.make_async_copy(table_ref.at[pl.ds(0, n_rows)], buf.at[slot], sem.at[slot]).wait()

    x = x_ref[...]
    xb = x.astype(BF16)
    tok = lax.broadcasted_iota(jnp.int32, (tt, 2 * PEER_SLOTS), 0)

    def token_rows(ti):
        return pltpu.bitcast(buf[slot, pl.ds(ti * PEER_SLOTS, PEER_SLOTS), :], BF16)

    act = jnp.zeros((tt, 2 * PEER_SLOTS), F32)
    for ti in range(tt):
        act = jnp.where(tok == ti, _dot_nt(xb, token_rows(ti)), act)
    w = (_gelu_tanh(pltpu.roll(act, 1, 1)) * g2_ref[...]).astype(BF16)
    ffn = jnp.zeros((tt, D_MODEL), F32)
    for ti in range(tt):
        ffn = ffn + _dot(jnp.where(tok == ti, w, jnp.zeros_like(w)), token_rows(ti))
    o_ref[...] = _layer_norm_rows(DN_ALPHA * x + ffn, lng_ref[...], lnb_ref[...])


def _peer_experts(h, eidx, gates2, table, ln_g, ln_b):
    n = h.shape[0]
    tt = PEER_TT
    n_rows = tt * PEER_SLOTS
    n_steps = n // tt
    idx = eidx.reshape(n_steps, 1, n_rows)
    return pl.pallas_call(
        _peer_expert_kernel,
        out_shape=jax.ShapeDtypeStruct((n, D_MODEL), F32),
        grid=(n_steps,),
        in_specs=[pl.BlockSpec((1, 1, n_rows), lambda i: (i, 0, 0), memory_space=pltpu.SMEM),
                  pl.BlockSpec((1, 1, n_rows), lambda i: (jnp.minimum(i + 1, n_steps - 1), 0, 0),
                               memory_space=pltpu.SMEM),
                  pl.BlockSpec((tt, D_MODEL), lambda i: (i, 0)),
                  pl.BlockSpec((tt, 2 * PEER_SLOTS), lambda i: (i, 0)),
                  pl.BlockSpec(memory_space=pl.ANY),
                  pl.BlockSpec((1, D_MODEL), lambda i: (0, 0)),
                  pl.BlockSpec((1, D_MODEL), lambda i: (0, 0))],
        out_specs=pl.BlockSpec((tt, D_MODEL), lambda i: (i, 0)),
        scratch_shapes=[pltpu.VMEM((2, n_rows, D_MODEL), jnp.uint32), pltpu.SemaphoreType.DMA((2,))],
        compiler_params=_cparams(("arbitrary",)),
        name="peer_experts",
    )(idx, idx, h, gates2, table, ln_g.reshape(1, D_MODEL), ln_b.reshape(1, D_MODEL))


def _pack_uv(u, v):
    ub = lax.bitcast_convert_type(u.astype(BF16), jnp.uint16).astype(jnp.uint32)
    vb = lax.bitcast_convert_type(v.astype(BF16), jnp.uint16).astype(jnp.uint32)
    return ub | (vb << 16)


def _peer_layer(h, w_q, subkeys, u, v, ln_g, ln_b):
    n = h.shape[0]
    eidx_t, gate_t = _peer_route(h, w_q, subkeys)
    eidx = eidx_t.T
    gates2 = jnp.stack([jnp.zeros((n, PEER_SLOTS), F32), gate_t.T], axis=-1).reshape(n, 2 * PEER_SLOTS)
    return _peer_experts(h, eidx, gates2, _pack_uv(u, v), ln_g, ln_b)


def kernel(x, nsa_w_in, nsa_cmp_k_w1, nsa_cmp_k_w2, nsa_cmp_k_pe, nsa_cmp_v_w1, nsa_cmp_v_w2, nsa_cmp_v_pe, nsa_w_out, nsa_ln_g, nsa_ln_b, mla_w_dkv, mla_kv_norm_g, mla_w_kr, mla_w_uk, mla_w_uv, mla_w_dq, mla_q_norm_g, mla_w_uq, mla_w_out, mla_ln_g, mla_ln_b, peer_w_q, peer_subkeys, peer_u, peer_v, peer_ln_g, peer_ln_b):
    b, s, d = x.shape
    h = x.reshape(b * s, d)
    k_t = v_t = None
    for layer in range(DEPTH):
        if layer < N_A_LAYERS:
            a = layer
            h = _nsa_layer(h, b, s, nsa_w_in[a], nsa_cmp_k_w1[a], nsa_cmp_k_w2[a], nsa_cmp_k_pe[a],
                           nsa_cmp_v_w1[a], nsa_cmp_v_w2[a], nsa_cmp_v_pe[a], nsa_w_out[a], nsa_ln_g[a], nsa_ln_b[a])
        else:
            if layer == N_A_LAYERS:
                k_t, v_t = _mla_shared_kv(h, b, s, mla_w_dkv, mla_kv_norm_g, mla_w_kr, mla_w_uk, mla_w_uv)
            bi = layer - N_A_LAYERS
            h = _mla_layer(h, b, s, k_t, v_t, mla_w_dq[bi], mla_q_norm_g[bi], mla_w_uq[bi], mla_w_out[bi],
                           mla_ln_g[bi], mla_ln_b[bi])
        h = _peer_layer(h, peer_w_q[layer], peer_subkeys[layer], peer_u[layer], peer_v[layer],
                        peer_ln_g[layer], peer_ln_b[layer])
    return h.reshape(b, s, d)
```
